```python
import math
import jax
import jax.numpy as jnp
from jax import lax
import numpy as np

D_MODEL = 1024
BATCH = 16
SEQ = 2048
DEPTH = 2

N_EVEN = (DEPTH + 1) // 2
N_ODD = DEPTH // 2
SB_HEADS = 8
SB_HEAD_DIM = 64
SB_WIDTH = SB_HEADS * SB_HEAD_DIM
QUERY_BLOCK = 128
POOL_WINDOWS = (2, 4, 8, 16)
POOL_WIDTH = D_MODEL - SB_WIDTH
POOL_GROUP = POOL_WIDTH // len(POOL_WINDOWS)
AB_IN_WIDTH = 3 * SB_WIDTH + POOL_WIDTH
SSM_WIDTH = D_MODEL
SSM_GROUP = 16
SSM_GROUPS = SSM_WIDTH // SSM_GROUP
SSM_STATE = 64
DT_MIN = 1e-3
DT_MAX = 1e-1
MEM_LEN = 256
XA_HEADS = 4
XA_HEAD_DIM = D_MODEL // XA_HEADS
D_FF = 2816
CONV_WIDTH = 3
EPS = 1e-6

kernel_name = "hybrid_stickbreak_pool_s5_block"


def rmsnorm(x, g):
    xf = x.astype(jnp.float32)
    xf = xf * lax.rsqrt(jnp.mean(xf * xf, axis=-1, keepdims=True) + EPS)
    return (xf * g.astype(jnp.float32)).astype(x.dtype)


def stick_breaking_attention(q, k, v):
    seq = q.shape[1]
    scale = q.shape[-1] ** -0.5
    outs = []
    for t0 in range(0, seq, QUERY_BLOCK):
        t1 = t0 + QUERY_BLOCK
        z = jnp.einsum('bqhd,bkhd->bhqk', q[:, t0:t1], k[:, :t1]).astype(jnp.float32) * scale
        causal = jnp.arange(t1)[None, :] < (t0 + jnp.arange(QUERY_BLOCK))[:, None]
        log_beta = jax.nn.log_sigmoid(z)
        log_keep = jnp.where(causal, log_beta - z, 0.0)
        after = lax.cumsum(log_keep, axis=3, reverse=True) - log_keep
        w = jnp.where(causal, jnp.exp(log_beta + after), 0.0)
        outs.append(jnp.einsum('bhqk,bkhd->bqhd', w.astype(v.dtype), v[:, :t1]))
    return jnp.concatenate(outs, axis=1)


def multiscale_pool(u, w_grp, scale):
    bsz, seq, _ = u.shape
    ug = u.astype(jnp.float32).reshape(bsz, seq, len(POOL_WINDOWS), POOL_GROUP)
    cs = jnp.concatenate([jnp.zeros_like(ug[:, :1]), jnp.cumsum(ug, axis=1)], axis=1)
    t = jnp.arange(seq)
    pooled = []
    for g, win in enumerate(POOL_WINDOWS):
        cs_g = cs[:, :, g]
        lo = jnp.maximum(t + 1 - win, 0)
        cnt = jnp.minimum(t + 1, win).astype(jnp.float32)[None, :, None]
        mean = (cs_g[:, 1:] - cs_g[:, lo]) / cnt
        pooled.append(mean - ug[:, :, g])
    p = jnp.stack(pooled, axis=2)
    y = jnp.einsum('bsgc,gcd->bsgd', p, w_grp.astype(jnp.float32)).reshape(bsz, seq, POOL_WIDTH)
    return (y * scale.astype(jnp.float32)).astype(u.dtype)


def _complex_linear_combine(left, right):
    a1r, a1i, b1r, b1i = left
    a2r, a2i, b2r, b2i = right
    ar = a1r * a2r - a1i * a2i
    ai = a1r * a2i + a1i * a2r
    br = a2r * b1r - a2i * b1i + b2r
    bi = a2r * b1i + a2i * b1r + b2i
    return (ar, ai, br, bi)


def s5_ssm(u, lam_re, lam_im, log_dt, b_re, b_im, c_re, c_im, d_skip):
    bsz, seq, _ = u.shape
    f32 = jnp.float32
    uf = u.astype(f32)
    ug = uf.reshape(bsz, seq, SSM_GROUPS, SSM_GROUP)
    lam_re = lam_re.astype(f32)
    lam_im = lam_im.astype(f32)
    dt = jnp.exp(log_dt.astype(f32))[:, None]
    mag = jnp.exp(lam_re * dt)
    ang = lam_im * dt
    lb_re = mag * jnp.cos(ang)
    lb_im = mag * jnp.sin(ang)
    n_re = lb_re - 1.0
    den = lam_re * lam_re + lam_im * lam_im
    coef_re = (n_re * lam_re + lb_im * lam_im) / den
    coef_im = (lb_im * lam_re - n_re * lam_im) / den
    b_re = b_re.astype(f32)
    b_im = b_im.astype(f32)
    bb_re = coef_re[..., None] * b_re - coef_im[..., None] * b_im
    bb_im = coef_re[..., None] * b_im + coef_im[..., None] * b_re
    bu_re = jnp.einsum('bsgc,gpc->bsgp', ug, bb_re)
    bu_im = jnp.einsum('bsgc,gpc->bsgp', ug, bb_im)
    a_re = jnp.broadcast_to(lb_re, (1, seq) + lb_re.shape)
    a_im = jnp.broadcast_to(lb_im, (1, seq) + lb_im.shape)
    _, _, h_re, h_im = lax.associative_scan(
        _complex_linear_combine, (a_re, a_im, bu_re, bu_im), axis=1)
    y = (jnp.einsum('bsgp,gcp->bsgc', h_re, c_re.astype(f32))
         - jnp.einsum('bsgp,gcp->bsgc', h_im, c_im.astype(f32)))
    return y.reshape(bsz, seq, SSM_WIDTH) + d_skip.astype(f32) * uf


def memory_cross_attention(h, mem_n, w_q, w_kv, w_o):
    bsz, seq, _ = h.shape
    m = mem_n.shape[1]
    q = (h @ w_q).reshape(bsz, seq, XA_HEADS, XA_HEAD_DIM)
    k, v = jnp.split(mem_n @ w_kv, 2, axis=-1)
    k = k.reshape(bsz, m, XA_HEADS, XA_HEAD_DIM)
    v = v.reshape(bsz, m, XA_HEADS, XA_HEAD_DIM)
    scores = jnp.einsum('bshd,bmhd->bhsm', q, k).astype(jnp.float32) * (XA_HEAD_DIM ** -0.5)
    p = jax.nn.softmax(scores, axis=-1).astype(v.dtype)
    o = jnp.einsum('bhsm,bmhd->bshd', p, v).reshape(bsz, seq, D_MODEL)
    return o @ w_o


def conv_gated_mlp(h, w_up, conv_w, conv_b, w_down):
    up = h @ w_up
    seq = up.shape[1]
    padded = jnp.pad(up, ((0, 0), (CONV_WIDTH - 1, 0), (0, 0)))
    conv = conv_b
    for i in range(CONV_WIDTH):
        conv = conv + conv_w[i] * padded[:, i:i + seq]
    val, gate = jnp.split(conv, 2, axis=-1)
    return (jax.nn.silu(gate) * val) @ w_down


def setup_inputs(seed: int = 0) -> dict:
    key = jax.random.key(seed)
    ks = iter(jax.random.split(key, 40))

    def nrm(shape, fan_in):
        return jax.random.normal(next(ks), shape, jnp.float32) * (fan_in ** -0.5)

    def gain(shape):
        return 1.0 + 0.02 * jax.random.normal(next(ks), shape, jnp.float32)

    n_arange = jnp.arange(SSM_STATE, dtype=jnp.float32)
    return {
        "x": jax.random.normal(next(ks), (BATCH, SEQ, D_MODEL), jnp.float32),
        "mem": jax.random.normal(next(ks), (BATCH, MEM_LEN, D_MODEL), jnp.float32),
        "norm_mix": gain((DEPTH, D_MODEL)),
        "norm_xattn": gain((DEPTH, D_MODEL)),
        "norm_ffn": gain((DEPTH, D_MODEL)),
        "norm_mem": gain((D_MODEL,)),
        "norm_final": gain((D_MODEL,)),
        "ab_w_in": nrm((N_EVEN, D_MODEL, AB_IN_WIDTH), D_MODEL),
        "pool_w": nrm((N_EVEN, len(POOL_WINDOWS), POOL_GROUP, POOL_GROUP), POOL_GROUP),
        "pool_scale": gain((N_EVEN, POOL_WIDTH)),
        "ab_w_out": nrm((N_EVEN, SB_WIDTH + POOL_WIDTH, D_MODEL), SB_WIDTH + POOL_WIDTH),
        "ssm_w_in": nrm((N_ODD, D_MODEL, SSM_WIDTH), D_MODEL),
        "ssm_lam_re": -0.5 + 0.01 * jax.random.normal(next(ks), (N_ODD, SSM_GROUPS, SSM_STATE), jnp.float32),
        "ssm_lam_im": math.pi * n_arange + 0.01 * jax.random.normal(next(ks), (N_ODD, SSM_GROUPS, SSM_STATE), jnp.float32),
        "ssm_log_dt": jax.random.uniform(next(ks), (N_ODD, SSM_GROUPS), jnp.float32,
                                         math.log(DT_MIN), math.log(DT_MAX)),
        "ssm_b_re": nrm((N_ODD, SSM_GROUPS, SSM_STATE, SSM_GROUP), 2 * SSM_GROUP),
        "ssm_b_im": nrm((N_ODD, SSM_GROUPS, SSM_STATE, SSM_GROUP), 2 * SSM_GROUP),
        "ssm_c_re": nrm((N_ODD, SSM_GROUPS, SSM_GROUP, SSM_STATE), SSM_STATE),
        "ssm_c_im": nrm((N_ODD, SSM_GROUPS, SSM_GROUP, SSM_STATE), SSM_STATE),
        "ssm_d": jax.random.normal(next(ks), (N_ODD, SSM_WIDTH), jnp.float32),
        "ssm_w_glu": nrm((N_ODD, SSM_WIDTH, 2 * D_MODEL), SSM_WIDTH),
        "xa_w_q": nrm((DEPTH, D_MODEL, D_MODEL), D_MODEL),
        "xa_w_kv": nrm((DEPTH, D_MODEL, 2 * D_MODEL), D_MODEL),
        "xa_w_o": nrm((DEPTH, D_MODEL, D_MODEL), D_MODEL),
        "ffn_w_up": nrm((DEPTH, D_MODEL, 2 * D_FF), D_MODEL),
        "ffn_conv_w": nrm((DEPTH, CONV_WIDTH, 2 * D_FF), CONV_WIDTH),
        "ffn_conv_b": 0.01 * jax.random.normal(next(ks), (DEPTH, 2 * D_FF), jnp.float32),
        "ffn_w_down": nrm((DEPTH, D_FF, D_MODEL), D_FF),
    }


def reference(x, mem, norm_mix, norm_xattn, norm_ffn, norm_mem, norm_final,
              ab_w_in, pool_w, pool_scale, ab_w_out,
              ssm_w_in, ssm_lam_re, ssm_lam_im, ssm_log_dt, ssm_b_re, ssm_b_im,
              ssm_c_re, ssm_c_im, ssm_d, ssm_w_glu,
              xa_w_q, xa_w_kv, xa_w_o,
              ffn_w_up, ffn_conv_w, ffn_conv_b, ffn_w_down):
    bsz, seq, _ = x.shape
    mem_n = rmsnorm(mem, norm_mem)
    for layer in range(DEPTH):
        h = rmsnorm(x, norm_mix[layer])
        if layer % 2 == 0:
            e = layer // 2
            proj = h @ ab_w_in[e]
            q, k, v, u = jnp.split(proj, [SB_WIDTH, 2 * SB_WIDTH, 3 * SB_WIDTH], axis=-1)
            q = q.reshape(bsz, seq, SB_HEADS, SB_HEAD_DIM)
            k = k.reshape(bsz, seq, SB_HEADS, SB_HEAD_DIM)
            v = v.reshape(bsz, seq, SB_HEADS, SB_HEAD_DIM)
            a_out = stick_breaking_attention(q, k, v).reshape(bsz, seq, SB_WIDTH)
            p_out = multiscale_pool(u, pool_w[e], pool_scale[e])
            mix = jnp.concatenate([a_out, p_out], axis=-1) @ ab_w_out[e]
        else:
            o = layer // 2
            u = h @ ssm_w_in[o]
            y = s5_ssm(u, ssm_lam_re[o], ssm_lam_im[o], ssm_log_dt[o], ssm_b_re[o],
                       ssm_b_im[o], ssm_c_re[o], ssm_c_im[o], ssm_d[o])
            glu = jax.nn.gelu(y).astype(x.dtype) @ ssm_w_glu[o]
            val, gate = jnp.split(glu, 2, axis=-1)
            mix = val * jax.nn.sigmoid(gate)
        x = x + mix
        x = x + memory_cross_attention(rmsnorm(x, norm_xattn[layer]), mem_n,
                                       xa_w_q[layer], xa_w_kv[layer], xa_w_o[layer])
        x = x + conv_gated_mlp(rmsnorm(x, norm_ffn[layer]), ffn_w_up[layer],
                               ffn_conv_w[layer], ffn_conv_b[layer], ffn_w_down[layer])
    return rmsnorm(x, norm_final)
```

```python
import functools

import jax
import jax.numpy as jnp
from jax import lax
from jax.experimental import pallas as pl
from jax.experimental.pallas import tpu as pltpu

F32 = jnp.float32
BF16 = jnp.bfloat16

EPS = 1e-6
LANES = 128
SB_HEADS = 8
SB_HEAD_DIM = 64
SB_WIDTH = SB_HEADS * SB_HEAD_DIM
POOL_WINDOWS = (2, 4, 8, 16)
POOL_GROUP = 128
POOL_WIDTH = POOL_GROUP * len(POOL_WINDOWS)
POOL_HALO = 16
SSM_GROUP = 16
SSM_STATE = 64
SSM_SLAB_GROUPS = LANES // SSM_GROUP
SSM_SLAB_STATES = SSM_SLAB_GROUPS * SSM_STATE
XA_HEADS = 4
CONV_WIDTH = 3
FFN_HALO = 16
VMEM_LIMIT = 56 * 1024 * 1024


def _rms(x, g):
    return x * lax.rsqrt(jnp.mean(x * x, axis=-1, keepdims=True) + EPS) * g


def _dot(a, b):
    return jnp.dot(a, b, preferred_element_type=F32)


def _dot_nt(a, b):
    return lax.dot_general(a, b, (((1,), (1,)), ((), ())), preferred_element_type=F32)


def _params(sem, vmem=VMEM_LIMIT):
    return pltpu.CompilerParams(dimension_semantics=sem, vmem_limit_bytes=vmem)


def _const_spec(shape):
    nd = len(shape)
    return pl.BlockSpec(shape, lambda *_: (0,) * nd, pipeline_mode=pl.Buffered(1))


def _ln_proj_kernel(x_ref, g_ref, w_ref, *refs, splits, col_chunk):
    out_refs, h_ref = refs[:-1], refs[-1]
    h_ref[...] = _rms(x_ref[...], g_ref[...]).astype(BF16)
    start = 0
    for o_ref, width in zip(out_refs, splits):
        for c0 in range(0, width, col_chunk):
            c1 = min(c0 + col_chunk, width)
            o_ref[:, c0:c1] = _dot(h_ref[...], w_ref[:, start + c0:start + c1]).astype(o_ref.dtype)
        start += width


def _ln_proj(x2d, g, w, splits, dtypes, tm, name):
    n, d = x2d.shape
    kern = functools.partial(_ln_proj_kernel, splits=tuple(splits), col_chunk=512)
    return pl.pallas_call(
        kern,
        out_shape=[jax.ShapeDtypeStruct((n, s), dt) for s, dt in zip(splits, dtypes)],
        grid=(n // tm,),
        in_specs=[pl.BlockSpec((tm, d), lambda i: (i, 0)),
                  _const_spec((1, d)),
                  _const_spec(w.shape)],
        out_specs=[pl.BlockSpec((tm, s), lambda i: (i, 0)) for s in splits],
        scratch_shapes=[pltpu.VMEM((tm, d), BF16)],
        compiler_params=_params(("arbitrary",)),
        name=name,
    )(x2d, g.reshape(1, d), w)


def _sb_attn_kernel(q_ref, k_ref, v_ref, mm_ref, o_ref, qm_ref, vv_ref, acc_ref, carry_ref, *, bq, bk):
    i = pl.program_id(2)
    nkb = k_ref.shape[1] // bk
    sub = bq // bk
    lane = lax.broadcasted_iota(jnp.int32, (bq, LANES), 1)
    head0 = lane < SB_HEAD_DIM

    q = q_ref[0] * jnp.asarray(SB_HEAD_DIM ** -0.5, BF16)
    zero = jnp.zeros_like(q)
    qm_ref[0:bq, :] = jnp.where(head0, q, zero)
    qm_ref[bq:2 * bq, :] = jnp.where(head0, zero, q)

    @pl.when(i == 0)
    def _():
        lane_k = lax.broadcasted_iota(jnp.int32, (bk, LANES), 1) < SB_HEAD_DIM
        for jb in range(nkb):
            vb = v_ref[0, jb * bk:(jb + 1) * bk, :]
            zb = jnp.zeros_like(vb)
            vv_ref[jb, 0:bk, :] = jnp.where(lane_k, vb, zb)
            vv_ref[jb, bk:2 * bk, :] = jnp.where(lane_k, zb, vb)

    acc_ref[...] = jnp.zeros_like(acc_ref)
    carry_ref[...] = jnp.zeros_like(carry_ref)

    row = lax.broadcasted_iota(jnp.int32, (2 * bq, bk), 0) & (bq - 1)
    col = lax.broadcasted_iota(jnp.int32, (2 * bq, bk), 1)

    def key_block(j, mask):
        kb = k_ref[0, pl.ds(pl.multiple_of(j * bk, bk), bk), :]
        z = _dot_nt(qm_ref[...], kb)
        sp = jnp.maximum(z, 0.0) + jnp.log(1.0 + jnp.exp(-jnp.abs(z)))
        log_beta = z - sp
        log_keep = -sp
        if mask is not None:
            log_keep = jnp.where(mask, log_keep, 0.0)
        hi = log_keep.astype(BF16)
        lo = (log_keep - hi.astype(F32)).astype(BF16)
        sums = _dot(jnp.concatenate([hi, lo], axis=1), mm_ref[...])
        carry = carry_ref[...]
        w = jnp.exp(log_beta + sums[:, :bk] + carry)
        if mask is not None:
            w = jnp.where(mask, w, 0.0)
        carry_ref[...] = carry + sums[:, bk:]
        wb = w.astype(BF16)
        w2 = jnp.concatenate([wb[:bq], wb[bq:]], axis=1)
        acc_ref[...] += _dot(w2, vv_ref[j])

    for c in reversed(range(sub)):
        key_block(i * sub + c, (col + c * bk) < row)

    def body(n, _):
        key_block(i * sub - 1 - n, None)
        return 0

    lax.fori_loop(0, i * sub, body, 0)
    o_ref[0] = acc_ref[...].astype(o_ref.dtype)


def _sb_attention(qkv, bq=512, bk=128):
    b, s, _ = qkv.shape
    npair = SB_WIDTH // LANES
    tri = (jnp.arange(bk)[:, None] > jnp.arange(bk)[None, :]).astype(BF16)
    half = jnp.concatenate([tri, jnp.ones((bk, bk), BF16)], axis=1)
    mm = jnp.concatenate([half, half], axis=0)
    kern = functools.partial(_sb_attn_kernel, bq=bq, bk=bk)
    return pl.pallas_call(
        kern,
        out_shape=jax.ShapeDtypeStruct((b, s, SB_WIDTH), BF16),
        grid=(b, npair, s // bq),
        in_specs=[pl.BlockSpec((1, bq, LANES), lambda bi, p, i: (bi, i, p)),
                  pl.BlockSpec((1, s, LANES), lambda bi, p, i: (bi, 0, npair + p)),
                  pl.BlockSpec((1, s, LANES), lambda bi, p, i: (bi, 0, 2 * npair + p)),
                  _const_spec(mm.shape)],
        out_specs=pl.BlockSpec((1, bq, LANES), lambda bi, p, i: (bi, i, p)),
        scratch_shapes=[pltpu.VMEM((2 * bq, LANES), BF16),
                        pltpu.VMEM((s // bk, 2 * bk, LANES), BF16),
                        pltpu.VMEM((bq, LANES), F32),
                        pltpu.VMEM((2 * bq, bk), F32)],
        compiler_params=_params(("arbitrary", "arbitrary", "arbitrary")),
        name="sb_attention",
    )(qkv, qkv, qkv, mm)


def _pool_out_kernel(a_ref, u_ref, uh_ref, x_ref, pw_ref, ps_ref, wo_ref, o_ref, cat_ref, *, tt):
    i = pl.program_id(1)
    halo = uh_ref[0] * jnp.where(i > 0, 1.0, 0.0)
    ext = jnp.concatenate([halo, u_ref[0]], axis=0)
    t = i * tt + lax.broadcasted_iota(jnp.int32, (tt, 1), 0)
    cat_ref[:, 0:SB_WIDTH] = a_ref[0]
    for g, win in enumerate(POOL_WINDOWS):
        xg = ext[:, g * POOL_GROUP:(g + 1) * POOL_GROUP]
        s, span = xg, 1
        while span < win:
            s = s + pltpu.roll(s, span, 0)
            span *= 2
        cnt = jnp.minimum(t + 1, win).astype(F32)
        p = s[POOL_HALO:] / cnt - xg[POOL_HALO:]
        y = _dot(p.astype(BF16), pw_ref[g]) * ps_ref[:, g * POOL_GROUP:(g + 1) * POOL_GROUP]
        cat_ref[:, SB_WIDTH + g * POOL_GROUP:SB_WIDTH + (g + 1) * POOL_GROUP] = y.astype(BF16)
    o_ref[0] = x_ref[0] + _dot(cat_ref[...], wo_ref[...])


def _pool_out(a_out, u, x, pool_w, pool_scale, w_out, tt=512):
    b, s, d = x.shape
    hb = tt // POOL_HALO
    kern = functools.partial(_pool_out_kernel, tt=tt)
    return pl.pallas_call(
        kern,
        out_shape=jax.ShapeDtypeStruct((b, s, d), F32),
        grid=(b, s // tt),
        in_specs=[pl.BlockSpec((1, tt, SB_WIDTH), lambda bi, i: (bi, i, 0)),
                  pl.BlockSpec((1, tt, POOL_WIDTH), lambda bi, i: (bi, i, 0)),
                  pl.BlockSpec((1, POOL_HALO, POOL_WIDTH), lambda bi, i: (bi, jnp.maximum(i * hb - 1, 0), 0)),
                  pl.BlockSpec((1, tt, d), lambda bi, i: (bi, i, 0)),
                  _const_spec(pool_w.shape),
                  _const_spec((1, POOL_WIDTH)),
                  _const_spec(w_out.shape)],
        out_specs=pl.BlockSpec((1, tt, d), lambda bi, i: (bi, i, 0)),
        scratch_shapes=[pltpu.VMEM((tt, SB_WIDTH + POOL_WIDTH), BF16)],
        compiler_params=_params(("arbitrary", "arbitrary")),
        name="pool_out_proj",
    )(a_out, u, u, x, pool_w, pool_scale.reshape(1, POOL_WIDTH), w_out)


def _ssm_in_kernel(x_ref, g_ref, w_ref, o_ref):
    o_ref[...] = _dot(_rms(x_ref[0], g_ref[...]).astype(BF16), w_ref[...])


def _ssm_in(x, g, w_in, tt=512):
    b, s, d = x.shape
    return pl.pallas_call(
        _ssm_in_kernel,
        out_shape=jax.ShapeDtypeStruct((s, b * d), F32),
        grid=(b, s // tt),
        in_specs=[pl.BlockSpec((1, tt, d), lambda bi, i: (bi, i, 0)),
                  _const_spec((1, d)),
                  _const_spec(w_in.shape)],
        out_specs=pl.BlockSpec((tt, d), lambda bi, i: (i, bi)),
        compiler_params=_params(("arbitrary", "arbitrary")),
        name="ssm_in_proj",
    )(x, g.reshape(1, d), w_in)


def _ssm_scan_kernel(u_ref, wb_ref, wc_ref, lam_ref, d_ref, o_ref, state_ref, bu_ref, h_ref, *, ts, nb):
    ns = SSM_SLAB_STATES

    @pl.when(pl.program_id(0) == 0)
    def _():
        state_ref[...] = jnp.zeros_like(state_ref)

    for j in range(u_ref.shape[1] // LANES):
        cols = slice(j * LANES, (j + 1) * LANES)
        bu_ref[...] = _dot(u_ref[:, cols].astype(BF16), wb_ref[j])
        a_re = lam_ref[j, 0]
        a_im = lam_ref[j, 1]

        def step(t, carry):
            h_re, h_im = carry
            rows = pl.ds(pl.multiple_of(t * nb, nb), nb)
            n_re = a_re * h_re - a_im * h_im + bu_ref[rows, 0:ns]
            n_im = a_re * h_im + a_im * h_re + bu_ref[rows, ns:2 * ns]
            h_ref[rows, 0:ns] = n_re.astype(BF16)
            h_ref[rows, ns:2 * ns] = n_im.astype(BF16)
            return n_re, n_im

        h_re, h_im = lax.fori_loop(0, ts, step, (state_ref[j, :, 0:ns], state_ref[j, :, ns:2 * ns]), unroll=2)
        state_ref[j, :, 0:ns] = h_re
        state_ref[j, :, ns:2 * ns] = h_im
        y = _dot(h_ref[...], wc_ref[j]) + d_ref[:, cols] * u_ref[:, cols]
        o_ref[:, cols] = jax.nn.gelu(y, approximate=True).astype(o_ref.dtype)


def _ssm_scan(u_tm, wb, wc, lam, d_skip, nb, ts=32):
    n, d = u_tm.shape
    nslab = d // LANES
    rows = ts * nb
    kern = functools.partial(_ssm_scan_kernel, ts=ts, nb=nb)
    return pl.pallas_call(
        kern,
        out_shape=jax.ShapeDtypeStruct((n, d), BF16),
        grid=(n // rows,),
        in_specs=[pl.BlockSpec((rows, d), lambda i: (i, 0)),
                  _const_spec(wb.shape),
                  _const_spec(wc.shape),
                  _const_spec(lam.shape),
                  _const_spec((1, d))],
        out_specs=pl.BlockSpec((rows, d), lambda i: (i, 0)),
        scratch_shapes=[pltpu.VMEM((nslab, nb, 2 * SSM_SLAB_STATES), F32),
                        pltpu.VMEM((rows, 2 * SSM_SLAB_STATES), F32),
                        pltpu.VMEM((rows, 2 * SSM_SLAB_STATES), BF16)],
        compiler_params=_params(("arbitrary",)),
        name="ssm_scan",
    )(u_tm, wb, wc, lam, d_skip.reshape(1, d))


def _glu_kernel(g_ref, x_ref, w_ref, o_ref):
    d = x_ref.shape[2]
    glu = _dot(g_ref[...], w_ref[...])
    val, gate = glu[:, :d], glu[:, d:]
    o_ref[0] = x_ref[0] + val / (1.0 + jnp.exp(-gate))


def _glu_residual(g_tm, x, w_glu, tt=512):
    b, s, d = x.shape
    return pl.pallas_call(
        _glu_kernel,
        out_shape=jax.ShapeDtypeStruct((b, s, d), F32),
        grid=(b, s // tt),
        in_specs=[pl.BlockSpec((tt, d), lambda bi, i: (i, bi)),
                  pl.BlockSpec((1, tt, d), lambda bi, i: (bi, i, 0)),
                  _const_spec(w_glu.shape)],
        out_specs=pl.BlockSpec((1, tt, d), lambda bi, i: (bi, i, 0)),
        compiler_params=_params(("arbitrary", "arbitrary")),
        name="ssm_glu_residual",
    )(g_tm, x, w_glu)


def _ssm_discretise(lam_re, lam_im, log_dt, b_re, b_im, c_re, c_im, nb):
    g, p = lam_re.shape
    dt = jnp.exp(log_dt)[:, None]
    mag = jnp.exp(lam_re * dt)
    ang = lam_im * dt
    lb_re = mag * jnp.cos(ang)
    lb_im = mag * jnp.sin(ang)
    n_re = lb_re - 1.0
    den = lam_re * lam_re + lam_im * lam_im
    coef_re = (n_re * lam_re + lb_im * lam_im) / den
    coef_im = (lb_im * lam_re - n_re * lam_im) / den
    bb_re = coef_re[..., None] * b_re - coef_im[..., None] * b_im
    bb_im = coef_re[..., None] * b_im + coef_im[..., None] * b_re
    sg = SSM_SLAB_GROUPS
    nslab = g // sg
    eye = jnp.eye(sg, dtype=F32)

    def in_proj(bb):
        t = bb.transpose(0, 2, 1).reshape(nslab, sg, SSM_GROUP, p)
        return jnp.einsum('jgcp,gh->jgchp', t, eye).reshape(nslab, sg * SSM_GROUP, sg * p)

    def out_proj(c):
        t = c.transpose(0, 2, 1).reshape(nslab, sg, p, SSM_GROUP)
        return jnp.einsum('jgpc,gh->jgphc', t, eye).reshape(nslab, sg * p, sg * SSM_GROUP)

    wb = jnp.concatenate([in_proj(bb_re), in_proj(bb_im)], axis=2).astype(BF16)
    wc = jnp.concatenate([out_proj(c_re), -out_proj(c_im)], axis=1).astype(BF16)
    lam = jnp.stack([lb_re.reshape(nslab, sg * p), lb_im.reshape(nslab, sg * p)], axis=1)
    lam = jnp.broadcast_to(lam[:, :, None, :], (nslab, 2, nb, sg * p))
    return wb, wc, lam


def _xattn_kernel(x_ref, g_ref, wq_ref, kv_ref, wo_ref, o_ref, cat_ref):
    x = x_ref[0]
    d = x.shape[1]
    hd = d // XA_HEADS
    q = (_dot(_rms(x, g_ref[...]).astype(BF16), wq_ref[...]) * (hd ** -0.5)).astype(BF16)
    for h in range(XA_HEADS):
        cols = slice(h * hd, (h + 1) * hd)
        s = _dot_nt(q[:, cols], kv_ref[0, :, cols])
        e = jnp.exp(s - jnp.max(s, axis=-1, keepdims=True))
        o = _dot(e.astype(BF16), kv_ref[0, :, d + h * hd:d + (h + 1) * hd])
        cat_ref[:, cols] = (o / jnp.sum(e, axis=-1, keepdims=True)).astype(BF16)
    o_ref[0] = x + _dot(cat_ref[...], wo_ref[...])


def _xattn(x, g, w_q, kv, w_o, tt=512):
    b, s, d = x.shape
    m = kv.shape[1]
    return pl.pallas_call(
        _xattn_kernel,
        out_shape=jax.ShapeDtypeStruct((b, s, d), F32),
        grid=(b, s // tt),
        in_specs=[pl.BlockSpec((1, tt, d), lambda bi, i: (bi, i, 0)),
                  _const_spec((1, d)),
                  _const_spec(w_q.shape),
                  pl.BlockSpec((1, m, 2 * d), lambda bi, i: (bi, 0, 0)),
                  _const_spec(w_o.shape)],
        out_specs=pl.BlockSpec((1, tt, d), lambda bi, i: (bi, i, 0)),
        scratch_shapes=[pltpu.VMEM((tt, d), BF16)],
        compiler_params=_params(("arbitrary", "arbitrary")),
        name="mem_xattn",
    )(x, g.reshape(1, d), w_q, kv, w_o)


def _ffn_kernel(x_ref, xh_ref, g_ref, wuv_ref, wug_ref, cwv_ref, cwg_ref, cbv_ref, cbg_ref, wd_ref, gf_ref,
                o_ref, h_ref, acc_ref, *, final_norm):
    i = pl.program_id(1)
    x = x_ref[0]
    halo = xh_ref[0] * jnp.where(i > 0, 1.0, 0.0)
    h_ref[0:FFN_HALO, :] = _rms(halo, g_ref[...]).astype(BF16)
    h_ref[FFN_HALO:, :] = _rms(x, g_ref[...]).astype(BF16)
    acc_ref[...] = jnp.zeros_like(acc_ref)

    def causal_conv(up, w, bias):
        out = bias + w[2:3] * up[FFN_HALO:]
        out = out + w[1:2] * pltpu.roll(up, 1, 0)[FFN_HALO:]
        return out + w[0:1] * pltpu.roll(up, 2, 0)[FFN_HALO:]

    def chunk(c, _):
        h = h_ref[...]
        val = causal_conv(_dot(h, wuv_ref[c]), cwv_ref[c], cbv_ref[c])
        gate = causal_conv(_dot(h, wug_ref[c]), cwg_ref[c], cbg_ref[c])
        act = gate / (1.0 + jnp.exp(-gate)) * val
        acc_ref[...] += _dot(act.astype(BF16), wd_ref[c])
        return 0

    lax.fori_loop(0, wuv_ref.shape[0], chunk, 0)
    y = x + acc_ref[...]
    if final_norm:
        y = _rms(y, gf_ref[...])
    o_ref[0] = y


def _ffn(x, g, w_up, conv_w, conv_b, w_down, g_final, final_norm, tt=512, fc=256):
    b, s, d = x.shape
    f = w_down.shape[0]
    nc = f // fc
    hb = tt // FFN_HALO

    def cols(a):
        lead = a.shape[:-1]
        v = a[..., :f].reshape(lead + (nc, fc))
        gt = a[..., f:].reshape(lead + (nc, fc))
        return jnp.moveaxis(v, -2, 0), jnp.moveaxis(gt, -2, 0)

    wuv, wug = cols(w_up)
    cwv, cwg = cols(conv_w)
    cbv, cbg = cols(conv_b.reshape(1, 2 * f))
    wd = w_down.reshape(nc, fc, d)
    kern = functools.partial(_ffn_kernel, final_norm=final_norm)
    return pl.pallas_call(
        kern,
        out_shape=jax.ShapeDtypeStruct((b, s, d), F32),
        grid=(b, s // tt),
        in_specs=[pl.BlockSpec((1, tt, d), lambda bi, i: (bi, i, 0)),
                  pl.BlockSpec((1, FFN_HALO, d), lambda bi, i: (bi, jnp.maximum(i * hb - 1, 0), 0)),
                  _const_spec((1, d)),
                  _const_spec(wuv.shape), _const_spec(wug.shape),
                  _const_spec(cwv.shape), _const_spec(cwg.shape),
                  _const_spec(cbv.shape), _const_spec(cbg.shape),
                  _const_spec(wd.shape),
                  _const_spec((1, d))],
        out_specs=pl.BlockSpec((1, tt, d), lambda bi, i: (bi, i, 0)),
        scratch_shapes=[pltpu.VMEM((FFN_HALO + tt, d), BF16),
                        pltpu.VMEM((tt, d), F32)],
        compiler_params=_params(("arbitrary", "arbitrary")),
        name="conv_gated_mlp",
    )(x, x, g.reshape(1, d), wuv, wug, cwv, cwg, cbv, cbg, wd, g_final.reshape(1, d))


def kernel(x, mem, norm_mix, norm_xattn, norm_ffn, norm_mem, norm_final, ab_w_in, pool_w, pool_scale, ab_w_out, ssm_w_in, ssm_lam_re, ssm_lam_im, ssm_log_dt, ssm_b_re, ssm_b_im, ssm_c_re, ssm_c_im, ssm_d, ssm_w_glu, xa_w_q, xa_w_kv, xa_w_o, ffn_w_up, ffn_conv_w, ffn_conv_b, ffn_w_down):
    bsz, seq, d = x.shape
    mlen = mem.shape[1]
    depth = norm_mix.shape[0]
    bf = lambda a: a.astype(BF16)

    for layer in range(depth):
        if layer % 2 == 0:
            e = layer // 2
            qkv, u = _ln_proj(x.reshape(bsz * seq, d), norm_mix[layer], bf(ab_w_in[e]),
                              (3 * SB_WIDTH, POOL_WIDTH), (BF16, F32), tm=512, name="ab_in_proj")
            a_out = _sb_attention(qkv.reshape(bsz, seq, 3 * SB_WIDTH))
            x = _pool_out(a_out, u.reshape(bsz, seq, POOL_WIDTH), x, bf(pool_w[e]), pool_scale[e],
                          bf(ab_w_out[e]))
        else:
            o = layer // 2
            wb, wc, lam = _ssm_discretise(ssm_lam_re[o], ssm_lam_im[o], ssm_log_dt[o], ssm_b_re[o],
                                          ssm_b_im[o], ssm_c_re[o], ssm_c_im[o], bsz)
            u_tm = _ssm_in(x, norm_mix[layer], bf(ssm_w_in[o]))
            g_tm = _ssm_scan(u_tm.reshape(seq * bsz, d), wb, wc, lam, ssm_d[o], bsz)
            x = _glu_residual(g_tm.reshape(seq, bsz * d), x, bf(ssm_w_glu[o]))
        (kv,) = _ln_proj(mem.reshape(bsz * mlen, d), norm_mem, bf(xa_w_kv[layer]), (2 * d,), (BF16,),
                         tm=512, name="mem_kv_proj")
        x = _xattn(x, norm_xattn[layer], bf(xa_w_q[layer]), kv.reshape(bsz, mlen, 2 * d), bf(xa_w_o[layer]))
        x = _ffn(x, norm_ffn[layer], bf(ffn_w_up[layer]), ffn_conv_w[layer], ffn_conv_b[layer],
                 bf(ffn_w_down[layer]), norm_final, final_norm=(layer == depth - 1))
    return x
```
